```python
import jax, jax.numpy as jnp
from jax import lax
import numpy as np

D_MODEL = 2048
BATCH = 1
SEQ = 8192
DEPTH = 1

N_HEADS = 8
N_KV_HEADS = 2
HEAD_DIM = 128
ATTN_WIDTH = N_HEADS * HEAD_DIM
IDX_HEADS = 16
IDX_DIM = 64
INDEX_TOPK = 256
Q_BLOCK = 128
CONV_WIDTH = 1024
CONV_K = 3
N_EXPERTS = 64
N_GROUPS = 8
TOPK_GROUPS = 4
TOP_K = 8
EXPERT_FF = 512
SHARED_FF = 512
ROUTED_SCALE = 2.5
EXPERT_BLOCK = 128
PLE_DIM = 256
LN_EPS = 1e-5
DEEPNORM_ALPHA = (2 * DEPTH) ** 0.25
DEEPNORM_BETA = (8 * DEPTH) ** -0.25

IN_SPLITS = (ATTN_WIDTH,
             N_KV_HEADS * HEAD_DIM,
             N_KV_HEADS * HEAD_DIM,
             IDX_HEADS * IDX_DIM,
             IDX_DIM,
             IDX_HEADS,
             CONV_WIDTH,
             CONV_WIDTH,
             CONV_WIDTH,
             D_MODEL,
             D_MODEL)
IN_WIDTH = sum(IN_SPLITS)
SPLIT_POINTS = tuple(int(s) for s in np.cumsum(IN_SPLITS)[:-1])

kernel_name = "hybrid_dsa_shortconv_moe_block"


def layer_norm(x, g, b):
    xf = x.astype(jnp.float32)
    mu = jnp.mean(xf, axis=-1, keepdims=True)
    xc = xf - mu
    var = jnp.mean(jnp.square(xc), axis=-1, keepdims=True)
    y = xc * lax.rsqrt(var + LN_EPS) * g.astype(jnp.float32) + b.astype(jnp.float32)
    return y.astype(x.dtype)


def dsa_attention(q, k, v, q_idx, k_idx, w_idx):
    B, L = q.shape[0], q.shape[1]
    topk = min(INDEX_TOPK, L // 4)
    n_qb = L // Q_BLOCK
    q = q.reshape(B, L, N_KV_HEADS, N_HEADS // N_KV_HEADS, HEAD_DIM)
    w_idx = w_idx * (IDX_HEADS ** -0.5 * IDX_DIM ** -0.5)
    key_pos = jnp.arange(L, dtype=jnp.int32)
    attn_scale = HEAD_DIM ** -0.5
    gather = jax.vmap(lambda t, i: t[i])

    def block(qb):
        t0 = qb * Q_BLOCK
        q_pos = t0 + jnp.arange(Q_BLOCK, dtype=jnp.int32)
        qi = lax.dynamic_slice_in_dim(q_idx, t0, Q_BLOCK, axis=1)
        wi = lax.dynamic_slice_in_dim(w_idx, t0, Q_BLOCK, axis=1)
        rel = jax.nn.relu(jnp.einsum('bqhd,bsd->bqhs', qi, k_idx))
        score = jnp.einsum('bqhs,bqh->bqs', rel, wi).astype(jnp.float32)
        causal = key_pos[None, :] <= q_pos[:, None]
        score = jnp.where(causal[None], score, -jnp.inf)
        _, sel = lax.top_k(score, topk)
        valid = sel <= q_pos[None, :, None]
        kg = gather(k, sel)
        vg = gather(v, sel)
        qq = lax.dynamic_slice_in_dim(q, t0, Q_BLOCK, axis=1)
        logits = jnp.einsum('bqhgd,bqnhd->bqhgn', qq, kg).astype(jnp.float32) * attn_scale
        logits = jnp.where(valid[:, :, None, None, :], logits, -jnp.inf)
        probs = jax.nn.softmax(logits, axis=-1).astype(vg.dtype)
        out = jnp.einsum('bqhgn,bqnhd->bqhgd', probs, vg)
        return out.reshape(B, Q_BLOCK, ATTN_WIDTH)

    out = lax.map(block, jnp.arange(n_qb, dtype=jnp.int32))
    return out.transpose(1, 0, 2, 3).reshape(B, L, ATTN_WIDTH)


def short_conv(u, w):
    return lax.conv_general_dilated(
        u, w[:, None, :].astype(u.dtype), window_strides=(1,),
        padding=[(CONV_K - 1, 0)], dimension_numbers=('NWC', 'WIO', 'NWC'),
        feature_group_count=u.shape[-1])


def moe(h, router_w, router_b, w_gate, w_up, w_down, ws_gate, ws_up, ws_down):
    B, L, D = h.shape
    T = B * L
    hf = h.reshape(T, D)
    scores = jax.nn.sigmoid((hf @ router_w).astype(jnp.float32))
    biased = scores + router_b.astype(jnp.float32)
    grp = biased.reshape(T, N_GROUPS, N_EXPERTS // N_GROUPS)
    grp_score = lax.top_k(grp, 2)[0].sum(-1)
    _, top_grp = lax.top_k(grp_score, TOPK_GROUPS)
    grp_mask = jnp.any(top_grp[..., None] == jnp.arange(N_GROUPS)[None, None, :], axis=1)
    expert_mask = jnp.repeat(grp_mask, N_EXPERTS // N_GROUPS, axis=-1)
    cand = jnp.where(expert_mask, biased, -jnp.inf)
    _, sel = lax.top_k(cand, TOP_K)
    gate = jnp.take_along_axis(scores, sel, axis=-1)
    gate = gate / jnp.sum(gate, axis=-1, keepdims=True) * ROUTED_SCALE
    A = T * TOP_K
    flat_e = sel.reshape(A).astype(jnp.int32)
    flat_t = jnp.repeat(jnp.arange(T, dtype=jnp.int32), TOP_K)
    flat_g = gate.reshape(A)
    order = jnp.argsort(flat_e)
    e_sorted = flat_e[order]
    counts = jnp.bincount(flat_e, length=N_EXPERTS).astype(jnp.int32)
    start = jnp.cumsum(counts) - counts
    padded = (counts + EXPERT_BLOCK - 1) // EXPERT_BLOCK * EXPERT_BLOCK
    padded_end = jnp.cumsum(padded)
    padded_start = padded_end - padded
    dest = padded_start[e_sorted] + jnp.arange(A, dtype=jnp.int32) - start[e_sorted]
    n_blocks = -(-A // EXPERT_BLOCK) + N_EXPERTS
    P = n_blocks * EXPERT_BLOCK
    buf_tok = jnp.zeros((P,), jnp.int32).at[dest].set(flat_t[order])
    buf_gate = jnp.zeros((P,), h.dtype).at[dest].set(flat_g[order].astype(h.dtype))
    block_start = jnp.arange(n_blocks, dtype=jnp.int32) * EXPERT_BLOCK
    block_expert = jnp.minimum(jnp.searchsorted(padded_end, block_start, side='right'),
                               N_EXPERTS - 1)

    def expert_block(b):
        rows = lax.dynamic_slice_in_dim(buf_tok, b * EXPERT_BLOCK, EXPERT_BLOCK)
        g = lax.dynamic_slice_in_dim(buf_gate, b * EXPERT_BLOCK, EXPERT_BLOCK)
        e = block_expert[b]
        xb = hf[rows]
        hid = jax.nn.silu(xb @ w_gate[e]) * (xb @ w_up[e])
        return (hid @ w_down[e]) * g[:, None]

    y = lax.map(expert_block, jnp.arange(n_blocks, dtype=jnp.int32))
    routed = jnp.zeros_like(hf).at[buf_tok].add(y.reshape(P, D))
    shared = (jax.nn.silu(hf @ ws_gate) * (hf @ ws_up)) @ ws_down
    return (routed + shared).reshape(B, L, D)


def setup_inputs(seed: int = 0) -> dict:
    key = jax.random.key(seed)
    ks = jax.random.split(key, 24)
    f32 = jnp.float32

    def nrm(k, shape, fan_in, scale=1.0):
        return jax.random.normal(k, shape, f32) * (scale * fan_in ** -0.5)

    x = jax.random.normal(ks[0], (BATCH, SEQ, D_MODEL), f32)
    p = jax.random.normal(ks[1], (DEPTH, BATCH, SEQ, PLE_DIM), f32)
    w_in = nrm(ks[2], (DEPTH, D_MODEL, IN_WIDTH), D_MODEL)
    conv_w = nrm(ks[3], (DEPTH, CONV_K, CONV_WIDTH), CONV_K)
    w_attn_out = nrm(ks[4], (DEPTH, ATTN_WIDTH, D_MODEL), ATTN_WIDTH)
    w_conv_out = nrm(ks[5], (DEPTH, CONV_WIDTH, D_MODEL), CONV_WIDTH)
    w_out = nrm(ks[6], (DEPTH, D_MODEL, D_MODEL), D_MODEL, DEEPNORM_BETA)
    ln1_g = 1.0 + 0.02 * jax.random.normal(ks[7], (DEPTH, D_MODEL), f32)
    ln1_b = 0.02 * jax.random.normal(ks[8], (DEPTH, D_MODEL), f32)
    router_w = nrm(ks[9], (DEPTH, D_MODEL, N_EXPERTS), D_MODEL)
    router_b = 0.01 * jax.random.normal(ks[10], (DEPTH, N_EXPERTS), f32)
    w_gate = nrm(ks[11], (DEPTH, N_EXPERTS, D_MODEL, EXPERT_FF), D_MODEL)
    w_up = nrm(ks[12], (DEPTH, N_EXPERTS, D_MODEL, EXPERT_FF), D_MODEL)
    w_down = nrm(ks[13], (DEPTH, N_EXPERTS, EXPERT_FF, D_MODEL), EXPERT_FF, DEEPNORM_BETA)
    ws_gate = nrm(ks[14], (DEPTH, D_MODEL, SHARED_FF), D_MODEL)
    ws_up = nrm(ks[15], (DEPTH, D_MODEL, SHARED_FF), D_MODEL)
    ws_down = nrm(ks[16], (DEPTH, SHARED_FF, D_MODEL), SHARED_FF, DEEPNORM_BETA)
    w_ple_gate = nrm(ks[17], (DEPTH, D_MODEL, D_MODEL), D_MODEL)
    w_ple_proj = nrm(ks[18], (DEPTH, PLE_DIM, D_MODEL), PLE_DIM, DEEPNORM_BETA)
    ln2_g = 1.0 + 0.02 * jax.random.normal(ks[19], (DEPTH, D_MODEL), f32)
    ln2_b = 0.02 * jax.random.normal(ks[20], (DEPTH, D_MODEL), f32)
    return {"x": x, "p": p, "w_in": w_in, "conv_w": conv_w, "w_attn_out": w_attn_out,
            "w_conv_out": w_conv_out, "w_out": w_out, "ln1_g": ln1_g, "ln1_b": ln1_b,
            "router_w": router_w, "router_b": router_b, "w_gate": w_gate, "w_up": w_up,
            "w_down": w_down, "ws_gate": ws_gate, "ws_up": ws_up, "ws_down": ws_down,
            "w_ple_gate": w_ple_gate, "w_ple_proj": w_ple_proj, "ln2_g": ln2_g, "ln2_b": ln2_b}


def reference(x, p, w_in, conv_w, w_attn_out, w_conv_out, w_out, ln1_g, ln1_b,
              router_w, router_b, w_gate, w_up, w_down, ws_gate, ws_up, ws_down,
              w_ple_gate, w_ple_proj, ln2_g, ln2_b):
    B, L, _ = x.shape
    h = x
    for i in range(DEPTH):
        z = h @ w_in[i]
        (q, k, v, q_idx, k_idx, w_idx, c_b, c_c, c_h, g_attn, g_conv) = jnp.split(
            z, SPLIT_POINTS, axis=-1)
        y_attn = dsa_attention(
            q.reshape(B, L, N_HEADS, HEAD_DIM),
            k.reshape(B, L, N_KV_HEADS, HEAD_DIM),
            v.reshape(B, L, N_KV_HEADS, HEAD_DIM),
            q_idx.reshape(B, L, IDX_HEADS, IDX_DIM), k_idx, w_idx)
        y_conv = c_b * short_conv(c_c * c_h, conv_w[i])
        merged = (jax.nn.sigmoid(g_attn) * (y_attn @ w_attn_out[i])
                  + jax.nn.sigmoid(g_conv) * (y_conv @ w_conv_out[i]))
        h = layer_norm(DEEPNORM_ALPHA * h + merged @ w_out[i], ln1_g[i], ln1_b[i])
        ffn = moe(h, router_w[i], router_b[i], w_gate[i], w_up[i], w_down[i],
                  ws_gate[i], ws_up[i], ws_down[i])
        ple = jax.nn.sigmoid(h @ w_ple_gate[i]) * (p[i] @ w_ple_proj[i])
        h = layer_norm(DEEPNORM_ALPHA * h + ffn + ple, ln2_g[i], ln2_b[i])
    return h
```

```python
import functools

import jax
import jax.numpy as jnp
from jax import lax
from jax.experimental import pallas as pl
from jax.experimental.pallas import tpu as pltpu

F32 = jnp.float32
BF16 = jnp.bfloat16
I32 = jnp.int32

D_MODEL = 2048
N_HEADS = 8
N_KV_HEADS = 2
HEAD_DIM = 128
ATTN_WIDTH = N_HEADS * HEAD_DIM
IDX_HEADS = 16
IDX_DIM = 64
INDEX_TOPK = 256
CONV_WIDTH = 1024
CONV_K = 3
N_EXPERTS = 64
N_GROUPS = 8
GROUP_SIZE = N_EXPERTS // N_GROUPS
TOPK_GROUPS = 4
TOP_K = 8
EXPERT_FF = 512
SHARED_FF = 512
ROUTED_SCALE = 2.5
PLE_DIM = 256
LN_EPS = 1e-5
KV_WIDTH = N_KV_HEADS * HEAD_DIM
IN_SPLITS = (ATTN_WIDTH, KV_WIDTH, KV_WIDTH, IDX_HEADS * IDX_DIM, IDX_DIM, IDX_HEADS,
             CONV_WIDTH, CONV_WIDTH, CONV_WIDTH, D_MODEL, D_MODEL)

LANES = 128
SUBLANES = 8
VMEM_LIMIT_BYTES = 56 * 1024 * 1024
INT_MIN = -2 ** 31
NEG_INF = float("-inf")

MM_TM = 1024
DSA_TQ = 256
DSA_CK = 256
CONV_TC = 128
CONV_TR = 512
LN_TM = 256
ROUTER_TT = 512
ROW_TM = 128
EXPERT_ROWS = 256


def _cparams(*sem):
    return pltpu.CompilerParams(dimension_semantics=tuple(sem), vmem_limit_bytes=VMEM_LIMIT_BYTES)


def _mm_kernel(a_ref, w_ref, o_ref, *, act):
    acc = jnp.dot(a_ref[...], w_ref[...], preferred_element_type=F32)
    if act == "sigmoid":
        acc = jax.nn.sigmoid(acc)
    o_ref[...] = acc.astype(o_ref.dtype)


def _matmul(a, w, out_dtype, *, tn, act=None, name):
    m, k = a.shape
    n = w.shape[1]
    tm = min(MM_TM, m)
    assert m % tm == 0 and n % tn == 0
    return pl.pallas_call(
        functools.partial(_mm_kernel, act=act),
        out_shape=jax.ShapeDtypeStruct((m, n), out_dtype),
        grid=(m // tm, n // tn),
        in_specs=[pl.BlockSpec((tm, k), lambda i, j: (i, 0)),
                  pl.BlockSpec((k, tn), lambda i, j: (0, j))],
        out_specs=pl.BlockSpec((tm, tn), lambda i, j: (i, j)),
        compiler_params=_cparams("parallel", "arbitrary"),
        name=name,
    )(a, w)


def _sortable(bits):
    return bits ^ ((bits >> 31) & jnp.int32(0x7FFFFFFF))


def _dsa_kernel(q_ref, qi_ref, k_ref, vt_ref, ka_ref, kb_ref, wt_ref, o_ref,
                key_ref, acc_ref, m_ref, l_ref, thr_ref, cut_ref, *, tq, ck, topk):
    i = pl.program_id(0)
    nchunks = (i + 1) * (tq // ck)
    w = wt_ref[...] * (IDX_HEADS ** -0.5 * IDX_DIM ** -0.5)
    qpos = i * tq + lax.broadcasted_iota(I32, (1, tq), 1)
    nt = (((1,), (1,)), ((), ()))

    def kpos_of(k0):
        return k0 + lax.broadcasted_iota(I32, (ck, 1), 0)

    def idx_body(c, carry):
        k0 = pl.multiple_of(c * ck, ck)
        ka = ka_ref[pl.ds(k0, ck), :]
        kb = kb_ref[pl.ds(k0, ck), :]
        s = jnp.zeros((ck, tq), F32)
        for p in range(IDX_HEADS // 2):
            qp = qi_ref[:, p * LANES:(p + 1) * LANES]
            ra = lax.dot_general(ka, qp, nt, preferred_element_type=F32)
            rb = lax.dot_general(kb, qp, nt, preferred_element_type=F32)
            s = s + jnp.maximum(ra, 0.0) * w[2 * p:2 * p + 1, :]
            s = s + jnp.maximum(rb, 0.0) * w[2 * p + 1:2 * p + 2, :]
        key = _sortable(pltpu.bitcast(s, I32))
        key_ref[pl.ds(k0, ck), :] = jnp.where(kpos_of(k0) <= qpos, key, INT_MIN)
        return carry

    lax.fori_loop(0, nchunks, idx_body, 0)

    def count(pred):
        def body(c, acc):
            k0 = pl.multiple_of(c * ck, ck)
            hit = pred(key_ref[pl.ds(k0, ck), :], kpos_of(k0)).astype(I32)
            return acc + jnp.sum(hit.reshape(ck // SUBLANES, SUBLANES, tq), axis=0)
        acc = lax.fori_loop(0, nchunks, body, jnp.zeros((SUBLANES, tq), I32))
        return jnp.sum(acc, axis=0, keepdims=True)

    thr = jnp.full((1, tq), INT_MIN, I32)
    for bit in range(31, -1, -1):
        trial = jnp.zeros((1, tq), I32) if bit == 31 else thr | jnp.int32(1 << bit)
        cnt = count(lambda key, kpos, trial=trial: key >= trial)
        thr = jnp.where(cnt >= topk, trial, thr)
    thr = jnp.maximum(thr, INT_MIN + 1)
    thr_ref[...] = thr

    n_gt = count(lambda key, kpos: key > thr)
    n_eq = count(lambda key, kpos: key == thr)
    need = topk - n_gt
    cut_ref[...] = jnp.full((1, tq), 2 ** 30, I32)

    @pl.when(jnp.max(jnp.where(n_eq > need, 1, 0)) > 0)
    def _():
        cut = jnp.zeros((1, tq), I32)
        for bit in range(max(key_ref.shape[0] - 1, 1).bit_length() - 1, -1, -1):
            trial = cut | jnp.int32(1 << bit)
            cnt = count(lambda key, kpos, trial=trial: (key == thr) & (kpos < trial))
            cut = jnp.where(cnt < need, trial, cut)
        cut_ref[...] = jnp.where(n_eq > need, cut, 2 ** 30)

    scale = HEAD_DIM ** -0.5
    group = N_HEADS // N_KV_HEADS
    for h in range(N_HEADS):
        g = h // group
        qh = q_ref[:, h * HEAD_DIM:(h + 1) * HEAD_DIM]
        m_ref[...] = jnp.full((1, tq), NEG_INF, F32)
        l_ref[...] = jnp.zeros((1, tq), F32)
        acc_ref[...] = jnp.zeros((HEAD_DIM, tq), F32)

        def att_body(c, carry, g=g, qh=qh):
            k0 = pl.multiple_of(c * ck, ck)
            kc = k_ref[pl.ds(k0, ck), g * HEAD_DIM:(g + 1) * HEAD_DIM]
            s = lax.dot_general(kc, qh, nt, preferred_element_type=F32) * scale
            key = key_ref[pl.ds(k0, ck), :]
            t = thr_ref[...]
            sel = (key > t) | ((key == t) & (kpos_of(k0) <= cut_ref[...]))
            s = jnp.where(sel, s, NEG_INF)
            m_old = m_ref[...]
            m_new = jnp.maximum(m_old, jnp.max(s, axis=0, keepdims=True))
            m_safe = jnp.where(m_new == NEG_INF, 0.0, m_new)
            p = jnp.exp(s - m_safe)
            alpha = jnp.exp(m_old - m_safe)
            l_ref[...] = alpha * l_ref[...] + jnp.sum(p, axis=0, keepdims=True)
            vc = vt_ref[c, g * HEAD_DIM:(g + 1) * HEAD_DIM, :]
            pv = jnp.dot(vc, p.astype(BF16), preferred_element_type=F32)
            acc_ref[...] = alpha * acc_ref[...] + pv
            m_ref[...] = m_new
            return carry

        lax.fori_loop(0, nchunks, att_body, 0)
        o_ref[h * HEAD_DIM:(h + 1) * HEAD_DIM, :] = (acc_ref[...] / l_ref[...]).astype(o_ref.dtype)


def _dsa_attention(qkv, ka, kb, wt, vt3, topk):
    t = qkv.shape[0]
    tq = min(DSA_TQ, t)
    ck = min(DSA_CK, tq)
    assert t % tq == 0 and tq % ck == 0
    kern = functools.partial(_dsa_kernel, tq=tq, ck=ck, topk=topk)
    kcol = (ATTN_WIDTH + IDX_HEADS * IDX_DIM) // KV_WIDTH
    return pl.pallas_call(
        kern,
        out_shape=jax.ShapeDtypeStruct((ATTN_WIDTH, t), BF16),
        grid=(t // tq,),
        in_specs=[pl.BlockSpec((tq, ATTN_WIDTH), lambda i: (i, 0)),
                  pl.BlockSpec((tq, IDX_HEADS * IDX_DIM), lambda i: (i, 1)),
                  pl.BlockSpec((t, KV_WIDTH), lambda i: (0, kcol)),
                  pl.BlockSpec((t // ck, KV_WIDTH, ck), lambda i: (0, 0, 0)),
                  pl.BlockSpec((t, LANES), lambda i: (0, 0)),
                  pl.BlockSpec((t, LANES), lambda i: (0, 0)),
                  pl.BlockSpec((IDX_HEADS, tq), lambda i: (0, i))],
        out_specs=pl.BlockSpec((ATTN_WIDTH, tq), lambda i: (0, i)),
        scratch_shapes=[pltpu.VMEM((t, tq), I32),
                        pltpu.VMEM((HEAD_DIM, tq), F32),
                        pltpu.VMEM((1, tq), F32),
                        pltpu.VMEM((1, tq), F32),
                        pltpu.VMEM((1, tq), I32),
                        pltpu.VMEM((1, tq), I32)],
        compiler_params=_cparams("arbitrary"),
        name="dsa_attention",
    )(qkv, qkv, qkv, vt3, ka, kb, wt)


def _conv_kernel(cb_ref, cc_ref, ch_ref, w_ref, o_ref, u_ref, *, tr):
    t = cb_ref.shape[0]
    pad = SUBLANES
    u_ref[0:pad, :] = jnp.zeros((pad, u_ref.shape[1]), F32)

    def fill(r, carry):
        r0 = pl.multiple_of(r * tr, tr)
        u_ref[pl.ds(pad + r0, tr), :] = cc_ref[pl.ds(r0, tr), :] * ch_ref[pl.ds(r0, tr), :]
        return carry

    lax.fori_loop(0, t // tr, fill, 0)
    w0 = w_ref[0:1, :]
    w1 = w_ref[1:2, :]
    w2 = w_ref[2:3, :]

    def out(r, carry):
        r0 = pl.multiple_of(r * tr, tr)
        x = u_ref[pl.ds(r0, tr + pad), :]
        y = w0 * x[pad - 2:pad - 2 + tr] + w1 * x[pad - 1:pad - 1 + tr] + w2 * x[pad:pad + tr]
        o_ref[pl.ds(r0, tr), :] = (cb_ref[pl.ds(r0, tr), :] * y).astype(o_ref.dtype)
        return carry

    lax.fori_loop(0, t // tr, out, 0)


def _short_conv(c3, conv_w):
    t = c3.shape[0]
    tr = min(CONV_TR, t)
    nb = CONV_WIDTH // CONV_TC
    return pl.pallas_call(
        functools.partial(_conv_kernel, tr=tr),
        out_shape=jax.ShapeDtypeStruct((t, CONV_WIDTH), BF16),
        grid=(nb,),
        in_specs=[pl.BlockSpec((t, CONV_TC), lambda j: (0, j)),
                  pl.BlockSpec((t, CONV_TC), lambda j: (0, nb + j)),
                  pl.BlockSpec((t, CONV_TC), lambda j: (0, 2 * nb + j)),
                  pl.BlockSpec((CONV_K, CONV_TC), lambda j: (0, j))],
        out_specs=pl.BlockSpec((t, CONV_TC), lambda j: (0, j)),
        scratch_shapes=[pltpu.VMEM((t + SUBLANES, CONV_TC), F32)],
        compiler_params=_cparams("parallel"),
        name="short_conv",
    )(c3, c3, c3, conv_w)


def _merge_kernel(ya_ref, yc_ref, wa_ref, wc_ref, sa_ref, sc_ref, o_ref):
    a = jnp.dot(ya_ref[...], wa_ref[...], preferred_element_type=F32)
    c = jnp.dot(yc_ref[...], wc_ref[...], preferred_element_type=F32)
    o_ref[...] = (sa_ref[...] * a + sc_ref[...] * c).astype(o_ref.dtype)


def _merge(ya, yc, wa, wc, sig, *, tn=512):
    m = ya.shape[0]
    tm = min(MM_TM, m)
    nj = D_MODEL // tn
    return pl.pallas_call(
        _merge_kernel,
        out_shape=jax.ShapeDtypeStruct((m, D_MODEL), BF16),
        grid=(m // tm, nj),
        in_specs=[pl.BlockSpec((tm, ATTN_WIDTH), lambda i, j: (i, 0)),
                  pl.BlockSpec((tm, CONV_WIDTH), lambda i, j: (i, 0)),
                  pl.BlockSpec((ATTN_WIDTH, tn), lambda i, j: (0, j)),
                  pl.BlockSpec((CONV_WIDTH, tn), lambda i, j: (0, j)),
                  pl.BlockSpec((tm, tn), lambda i, j: (i, j)),
                  pl.BlockSpec((tm, tn), lambda i, j: (i, nj + j))],
        out_specs=pl.BlockSpec((tm, tn), lambda i, j: (i, j)),
        compiler_params=_cparams("parallel", "arbitrary"),
        name="gated_merge",
    )(ya, yc, wa, wc, sig, sig)


def _layer_norm(r, g, b):
    mu = jnp.mean(r, axis=-1, keepdims=True)
    rc = r - mu
    var = jnp.mean(rc * rc, axis=-1, keepdims=True)
    return rc * lax.rsqrt(var + LN_EPS) * g + b


def _outproj_kernel(m_ref, w_ref, x_ref, g_ref, b_ref, h_ref, hb_ref, *, alpha):
    acc = jnp.dot(m_ref[...], w_ref[...], preferred_element_type=F32)
    h = _layer_norm(alpha * x_ref[...] + acc, g_ref[...], b_ref[...])
    h_ref[...] = h
    hb_ref[...] = h.astype(BF16)


def _outproj_ln(m, w, x, g, b, alpha):
    t = m.shape[0]
    tm = min(LN_TM, t)
    row = pl.BlockSpec((tm, D_MODEL), lambda i: (i, 0))
    vec = pl.BlockSpec((1, D_MODEL), lambda i: (0, 0))
    return pl.pallas_call(
        functools.partial(_outproj_kernel, alpha=alpha),
        out_shape=(jax.ShapeDtypeStruct((t, D_MODEL), F32), jax.ShapeDtypeStruct((t, D_MODEL), BF16)),
        grid=(t // tm,),
        in_specs=[row, pl.BlockSpec((D_MODEL, D_MODEL), lambda i: (0, 0)), row, vec, vec],
        out_specs=(row, row),
        compiler_params=_cparams("parallel"),
        name="outproj_ln",
    )(m, w, x, g, b)


def _first_argmax(x, iota, n):
    m = jnp.max(x, axis=0, keepdims=True)
    idx = jnp.min(jnp.where(x == m, iota, n), axis=0, keepdims=True)
    return m, idx


def _stack_rows(rows, dtype):
    n = rows[0].shape[1]
    iota = lax.broadcasted_iota(I32, (len(rows), n), 0)
    out = jnp.zeros((len(rows), n), dtype)
    for k, row in enumerate(rows):
        out = jnp.where(iota == k, row.astype(dtype), out)
    return out


def _router_kernel(h_ref, rw_ref, rb_ref, sel_ref, gate_ref, rank_ref, cnt_ref, carry_ref, *, tt):
    i = pl.program_id(0)

    @pl.when(i == 0)
    def _():
        carry_ref[...] = jnp.zeros_like(carry_ref)

    nt = (((1,), (1,)), ((), ()))
    logits = lax.dot_general(rw_ref[...], h_ref[...], nt, preferred_element_type=F32)
    scores = jax.nn.sigmoid(logits)
    biased = scores + rb_ref[...]
    iota_g = lax.broadcasted_iota(I32, (GROUP_SIZE, tt), 0)
    gs = []
    for g in range(N_GROUPS):
        x = biased[g * GROUP_SIZE:(g + 1) * GROUP_SIZE, :]
        m1, i1 = _first_argmax(x, iota_g, GROUP_SIZE)
        m2 = jnp.max(jnp.where(iota_g == i1, NEG_INF, x), axis=0, keepdims=True)
        gs.append(m1 + m2)
    gscore = _stack_rows(gs, F32)
    iota_n = lax.broadcasted_iota(I32, (N_GROUPS, tt), 0)
    gmask = jnp.zeros((N_GROUPS, tt), F32)
    for _ in range(TOPK_GROUPS):
        _, gi = _first_argmax(gscore, iota_n, N_GROUPS)
        hit = iota_n == gi
        gmask = jnp.where(hit, 1.0, gmask)
        gscore = jnp.where(hit, NEG_INF, gscore)
    emask = jnp.concatenate(
        [jnp.broadcast_to(gmask[g:g + 1, :], (GROUP_SIZE, tt)) for g in range(N_GROUPS)], axis=0)
    cand = jnp.where(emask > 0.0, biased, NEG_INF)
    iota_e = lax.broadcasted_iota(I32, (N_EXPERTS, tt), 0)
    sels, gates = [], []
    onehot = jnp.zeros((N_EXPERTS, tt), F32)
    for _ in range(TOP_K):
        _, ei = _first_argmax(cand, iota_e, N_EXPERTS)
        hit = iota_e == ei
        sels.append(ei)
        gates.append(jnp.sum(jnp.where(hit, scores, 0.0), axis=0, keepdims=True))
        onehot = jnp.where(hit, 1.0, onehot)
        cand = jnp.where(hit, NEG_INF, cand)
    gate = _stack_rows(gates, F32)
    gate = gate / jnp.sum(gate, axis=0, keepdims=True) * ROUTED_SCALE
    sel_ref[...] = _stack_rows(sels, I32)
    gate_ref[...] = gate
    r = lax.broadcasted_iota(I32, (tt, tt), 0)
    c = lax.broadcasted_iota(I32, (tt, tt), 1)
    upper = jnp.where(r <= c, 1.0, 0.0).astype(BF16)
    incl = jnp.dot(onehot.astype(BF16), upper, preferred_element_type=F32)
    excl = incl - onehot + carry_ref[...]
    ranks = [jnp.sum(jnp.where(iota_e == s, excl, 0.0), axis=0, keepdims=True) for s in sels]
    rank_ref[...] = _stack_rows(ranks, F32).astype(I32)
    carry_ref[...] = carry_ref[...] + incl[:, tt - 1:tt]
    cnt_ref[...] = jnp.broadcast_to(carry_ref[...], cnt_ref.shape).astype(I32)


def _router(hb, rw_t, rb):
    t = hb.shape[0]
    tt = min(ROUTER_TT, t)
    tok = pl.BlockSpec((TOP_K, tt), lambda i: (0, i))
    return pl.pallas_call(
        functools.partial(_router_kernel, tt=tt),
        out_shape=(jax.ShapeDtypeStruct((TOP_K, t), I32), jax.ShapeDtypeStruct((TOP_K, t), F32),
                   jax.ShapeDtypeStruct((TOP_K, t), I32), jax.ShapeDtypeStruct((N_EXPERTS, LANES), I32)),
        grid=(t // tt,),
        in_specs=[pl.BlockSpec((tt, D_MODEL), lambda i: (i, 0)),
                  pl.BlockSpec((N_EXPERTS, D_MODEL), lambda i: (0, 0)),
                  pl.BlockSpec((N_EXPERTS, 1), lambda i: (0, 0))],
        out_specs=(tok, tok, tok, pl.BlockSpec((N_EXPERTS, LANES), lambda i: (0, 0))),
        scratch_shapes=[pltpu.VMEM((N_EXPERTS, 1), F32)],
        compiler_params=_cparams("arbitrary"),
        name="router",
    )(hb, rw_t, rb)


def _row_copy(src, dst, sem):
    return pltpu.make_async_copy(src, dst, sem)


def _dispatch_kernel(pstart_ref, pcnt_ref, ptot_ref, h_ref, sel_ref, rank_ref, xs_ref, z_ref, sem, zsem,
                     *, tm, br):
    i = pl.program_id(0)
    nsteps = pl.num_programs(0)

    for k in range(TOP_K):
        def issue(t, carry, k=k):
            d = pstart_ref[sel_ref[k, t]] + rank_ref[k, t]
            _row_copy(h_ref.at[pl.ds(t, 1), :], xs_ref.at[pl.ds(d, 1), :], sem).start()
            return carry
        lax.fori_loop(0, tm, issue, 0)

    @pl.when(i == nsteps - 1)
    def _():
        z_ref[...] = jnp.zeros_like(z_ref)
        for e in range(N_EXPERTS):
            n = pcnt_ref[e]
            start = pstart_ref[e] + n
            npad = (-n) % br

            def zero_row(r, carry, start=start):
                _row_copy(z_ref.at[pl.ds(0, 1), :], xs_ref.at[pl.ds(start + r, 1), :], zsem).start()
                return carry
            lax.fori_loop(0, npad, zero_row, 0)

            def zero_wait(r, carry):
                _row_copy(z_ref.at[pl.ds(0, 1), :], xs_ref.at[pl.ds(0, 1), :], zsem).wait()
                return carry
            lax.fori_loop(0, npad, zero_wait, 0)
        nblk = (xs_ref.shape[0] - ptot_ref[0]) // br
        first = ptot_ref[0] // br

        def zero_blk(b, carry):
            r0 = pl.multiple_of((first + b) * br, br)
            cp = _row_copy(z_ref, xs_ref.at[pl.ds(r0, br), :], zsem)
            cp.start()
            cp.wait()
            return carry
        lax.fori_loop(0, nblk, zero_blk, 0)

    for k in range(TOP_K):
        _row_copy(h_ref, xs_ref.at[pl.ds(0, tm), :], sem).wait()


def _dispatch(h, sel, rank, pstart, pcnt, ptot, nrows):
    t = h.shape[0]
    tm = min(ROW_TM, t)
    smem = functools.partial(pl.BlockSpec, memory_space=pltpu.SMEM)
    grid_spec = pltpu.PrefetchScalarGridSpec(
        num_scalar_prefetch=3,
        grid=(t // tm,),
        in_specs=[pl.BlockSpec((tm, D_MODEL), lambda i, *_: (i, 0)),
                  smem((TOP_K, tm), lambda i, *_: (0, i)),
                  smem((TOP_K, tm), lambda i, *_: (0, i))],
        out_specs=pl.BlockSpec(memory_space=pl.ANY),
        scratch_shapes=[pltpu.VMEM((EXPERT_ROWS, D_MODEL), F32),
                        pltpu.SemaphoreType.DMA, pltpu.SemaphoreType.DMA],
    )
    return pl.pallas_call(
        functools.partial(_dispatch_kernel, tm=tm, br=EXPERT_ROWS),
        out_shape=jax.ShapeDtypeStruct((nrows, D_MODEL), F32),
        grid_spec=grid_spec,
        compiler_params=_cparams("arbitrary"),
        name="moe_dispatch",
    )(pstart, pcnt, ptot, h, sel, rank)


def _ffn_kernel(bexp_ref, nused_ref, x_ref, wg_ref, wu_ref, wd_ref, o_ref, wgb, wub, wdb):
    b = pl.program_id(0)
    e = bexp_ref[b]
    prev = bexp_ref[jnp.maximum(b - 1, 0)]

    @pl.when((b == 0) | (e != prev))
    def _():
        wgb[...] = wg_ref[0].astype(BF16)
        wub[...] = wu_ref[0].astype(BF16)
        wdb[...] = wd_ref[0].astype(BF16)

    @pl.when(b < nused_ref[0])
    def _():
        x = x_ref[...].astype(BF16)
        g = jnp.dot(x, wgb[...], preferred_element_type=F32)
        u = jnp.dot(x, wub[...], preferred_element_type=F32)
        hid = (jax.nn.silu(g) * u).astype(BF16)
        o_ref[...] = jnp.dot(hid, wdb[...], preferred_element_type=F32)

    @pl.when(b >= nused_ref[0])
    def _():
        o_ref[...] = jnp.zeros_like(o_ref)


def _expert_ffn(xs, bexp, nused, wg, wu, wd):
    nrows = xs.shape[0]
    br = EXPERT_ROWS
    grid_spec = pltpu.PrefetchScalarGridSpec(
        num_scalar_prefetch=2,
        grid=(nrows // br,),
        in_specs=[pl.BlockSpec((br, D_MODEL), lambda b, be, nu: (jnp.minimum(b, nu[0] - 1), 0)),
                  pl.BlockSpec((1, D_MODEL, EXPERT_FF), lambda b, be, nu: (be[b], 0, 0)),
                  pl.BlockSpec((1, D_MODEL, EXPERT_FF), lambda b, be, nu: (be[b], 0, 0)),
                  pl.BlockSpec((1, EXPERT_FF, D_MODEL), lambda b, be, nu: (be[b], 0, 0))],
        out_specs=pl.BlockSpec((br, D_MODEL), lambda b, be, nu: (b, 0)),
        scratch_shapes=[pltpu.VMEM((D_MODEL, EXPERT_FF), BF16),
                        pltpu.VMEM((D_MODEL, EXPERT_FF), BF16),
                        pltpu.VMEM((EXPERT_FF, D_MODEL), BF16)],
    )
    return pl.pallas_call(
        _ffn_kernel,
        out_shape=jax.ShapeDtypeStruct((nrows, D_MODEL), F32),
        grid_spec=grid_spec,
        compiler_params=_cparams("arbitrary"),
        name="expert_ffn",
    )(bexp, nused, xs, wg, wu, wd)


def _tail_kernel(hb_ref, h_ref, p_ref, wsg_ref, wsu_ref, wsd_ref, wpg_ref, wpp_ref, o_ref, *, alpha):
    hb = hb_ref[...]
    g = jnp.dot(hb, wsg_ref[...], preferred_element_type=F32)
    u = jnp.dot(hb, wsu_ref[...], preferred_element_type=F32)
    hid = (jax.nn.silu(g) * u).astype(BF16)
    shared = jnp.dot(hid, wsd_ref[...], preferred_element_type=F32)
    pg = jax.nn.sigmoid(jnp.dot(hb, wpg_ref[...], preferred_element_type=F32))
    pp = jnp.dot(p_ref[...], wpp_ref[...], preferred_element_type=F32)
    o_ref[...] = alpha * h_ref[...] + shared + pg * pp


def _tail(hb, h, pb, wsg, wsu, wsd, wpg, wpp, alpha):
    t = hb.shape[0]
    tm = min(LN_TM, t)
    row = pl.BlockSpec((tm, D_MODEL), lambda i: (i, 0))

    def full(a):
        return pl.BlockSpec(a.shape, lambda i: (0, 0))

    return pl.pallas_call(
        functools.partial(_tail_kernel, alpha=alpha),
        out_shape=jax.ShapeDtypeStruct((t, D_MODEL), F32),
        grid=(t // tm,),
        in_specs=[row, row, pl.BlockSpec((tm, PLE_DIM), lambda i: (i, 0)),
                  full(wsg), full(wsu), full(wsd), full(wpg), full(wpp)],
        out_specs=row,
        compiler_params=_cparams("parallel"),
        name="shared_ple_tail",
    )(hb, h, pb, wsg, wsu, wsd, wpg, wpp)


def _combine_kernel(pstart_ref, r_ref, gate_ref, sel_ref, rank_ref, g_ref, b_ref, ys_ref, o_ref, buf, sem,
                    *, tm):
    for k in range(TOP_K):
        def issue(t, carry, k=k):
            d = pstart_ref[sel_ref[k, t]] + rank_ref[k, t]
            _row_copy(ys_ref.at[pl.ds(d, 1), :], buf.at[k, pl.ds(t, 1), :], sem).start()
            return carry
        lax.fori_loop(0, tm, issue, 0)
    for k in range(TOP_K):
        _row_copy(ys_ref.at[pl.ds(0, tm), :], buf.at[k], sem).wait()
    acc = r_ref[...]
    gate = gate_ref[...]
    for k in range(TOP_K):
        acc = acc + gate[:, k:k + 1] * buf[k]
    o_ref[...] = _layer_norm(acc, g_ref[...], b_ref[...])


def _combine(r, gate_rows, sel, rank, pstart, ys, g, b):
    t = r.shape[0]
    tm = min(ROW_TM, t)
    smem = functools.partial(pl.BlockSpec, memory_space=pltpu.SMEM)
    row = pl.BlockSpec((tm, D_MODEL), lambda i, *_: (i, 0))
    vec = pl.BlockSpec((1, D_MODEL), lambda i, *_: (0, 0))
    grid_spec = pltpu.PrefetchScalarGridSpec(
        num_scalar_prefetch=1,
        grid=(t // tm,),
        in_specs=[row, pl.BlockSpec((tm, TOP_K), lambda i, *_: (i, 0)),
                  smem((TOP_K, tm), lambda i, *_: (0, i)),
                  smem((TOP_K, tm), lambda i, *_: (0, i)),
                  vec, vec, pl.BlockSpec(memory_space=pl.ANY)],
        out_specs=row,
        scratch_shapes=[pltpu.VMEM((TOP_K, tm, D_MODEL), F32), pltpu.SemaphoreType.DMA],
    )
    return pl.pallas_call(
        functools.partial(_combine_kernel, tm=tm),
        out_shape=jax.ShapeDtypeStruct((t, D_MODEL), F32),
        grid_spec=grid_spec,
        compiler_params=_cparams("arbitrary"),
        name="moe_combine",
    )(pstart, r, gate_rows, sel, rank, g, b, ys)


def _split_w_in(w_in):
    offs = [0]
    for s in IN_SPLITS:
        offs.append(offs[-1] + s)
    return [w_in[:, offs[j]:offs[j + 1]] for j in range(len(IN_SPLITS))]


def _layer(h, p, w_in, conv_w, w_attn_out, w_conv_out, w_out, ln1_g, ln1_b, router_w, router_b,
           w_gate, w_up, w_down, ws_gate, ws_up, ws_down, w_ple_gate, w_ple_proj, ln2_g, ln2_b, alpha):
    t = h.shape[0]
    wq, wk, wv, wqi, wki, wwi, wcb, wcc, wch, wga, wgc = _split_w_in(w_in)
    hb = h.astype(BF16)
    w_a = jnp.concatenate([wq, wqi, wk, wv], axis=1).astype(BF16)
    w_i = jnp.concatenate([wki, wwi, jnp.zeros((D_MODEL, LANES - IDX_DIM - IDX_HEADS), F32)],
                          axis=1).astype(BF16)
    w_c = jnp.concatenate([wcb, wcc, wch], axis=1).astype(BF16)
    w_g = jnp.concatenate([wga, wgc], axis=1).astype(BF16)
    qkv = _matmul(hb, w_a, BF16, tn=512, name="in_proj_qkv")
    kiw = _matmul(hb, w_i, F32, tn=LANES, name="in_proj_idx")
    c3 = _matmul(hb, w_c, F32, tn=512, name="in_proj_conv")
    sig = _matmul(hb, w_g, F32, tn=512, act="sigmoid", name="in_proj_gates")
    ck = min(DSA_CK, min(DSA_TQ, t))
    kidx = kiw[:, :IDX_DIM].astype(BF16)
    zpad = jnp.zeros((t, LANES - IDX_DIM), BF16)
    ka = jnp.concatenate([kidx, zpad], axis=1)
    kb = jnp.concatenate([zpad, kidx], axis=1)
    wt = kiw[:, IDX_DIM:IDX_DIM + IDX_HEADS].T
    v = qkv[:, ATTN_WIDTH + IDX_HEADS * IDX_DIM + KV_WIDTH:]
    vt3 = v.reshape(t // ck, ck, KV_WIDTH).transpose(0, 2, 1)
    ya_t = _dsa_attention(qkv, ka, kb, wt, vt3, min(INDEX_TOPK, t // 4))
    ya = ya_t.T
    yc = _short_conv(c3, conv_w)
    m = _merge(ya, yc, w_attn_out.astype(BF16), w_conv_out.astype(BF16), sig)
    h1, h1b = _outproj_ln(m, w_out.astype(BF16), h, ln1_g[None, :], ln1_b[None, :], alpha)
    sel, gate, rank, cnt = _router(h1b, router_w.T.astype(BF16), router_b[:, None])
    counts = cnt[:, 0]
    br = EXPERT_ROWS
    padded = (counts + br - 1) // br * br
    pend = jnp.cumsum(padded)
    pstart = (pend - padded).astype(I32)
    ptot = pend[-1:].astype(I32)
    nrows = (t * TOP_K + N_EXPERTS * (br - 1)) // br * br
    nblocks = nrows // br
    bexp = jnp.minimum(jnp.searchsorted(pend, jnp.arange(nblocks, dtype=I32) * br, side="right"),
                       N_EXPERTS - 1).astype(I32)
    last_e = jnp.max(jnp.where(counts > 0, jnp.arange(N_EXPERTS, dtype=I32), 0))
    nused = (ptot // br).astype(I32)
    bexp = jnp.where(jnp.arange(nblocks) < nused[0], bexp, last_e).astype(I32)
    xs = _dispatch(h1, sel, rank, pstart, counts.astype(I32), ptot, nrows)
    ys = _expert_ffn(xs, bexp, nused, w_gate, w_up, w_down)
    r = _tail(h1b, h1, p.astype(BF16), ws_gate.astype(BF16), ws_up.astype(BF16), ws_down.astype(BF16),
              w_ple_gate.astype(BF16), w_ple_proj.astype(BF16), alpha)
    return _combine(r, gate.T, sel, rank, pstart, ys, ln2_g[None, :], ln2_b[None, :])


def kernel(x, p, w_in, conv_w, w_attn_out, w_conv_out, w_out, ln1_g, ln1_b, router_w, router_b, w_gate, w_up, w_down, ws_gate, ws_up, ws_down, w_ple_gate, w_ple_proj, ln2_g, ln2_b):
    b, l, d = x.shape
    depth = w_in.shape[0]
    assert b == 1 and d == D_MODEL
    alpha = (2 * depth) ** 0.25
    h = x.reshape(l, d)
    for i in range(depth):
        h = _layer(h, p[i].reshape(l, PLE_DIM), w_in[i], conv_w[i], w_attn_out[i], w_conv_out[i], w_out[i],
                   ln1_g[i], ln1_b[i], router_w[i], router_b[i], w_gate[i], w_up[i], w_down[i],
                   ws_gate[i], ws_up[i], ws_down[i], w_ple_gate[i], w_ple_proj[i], ln2_g[i], ln2_b[i], alpha)
    return h.reshape(b, l, d)
```
